```python
import jax
import jax.numpy as jnp
from jax import lax
import numpy as np

D_MODEL = 1024
BATCH = 8
SEQ = 2048
DEPTH = 2
DEC_BATCH = 128
DEC_SEQ = 1
PAST_LEN = 16384
PAGE_SIZE = 128

N_EVEN = (DEPTH + 1) // 2
N_ODD = DEPTH // 2
ALPHA = (2 * DEPTH) ** 0.25
BETA = (8 * DEPTH) ** -0.25
LN_EPS = 1e-5
MIX_WIDTH = D_MODEL
A_HEADS = 4
A_DK = 128
A_DV = 128
A_WIDTH = A_HEADS * A_DV
A_CHUNK = 64
B_WIDTH = MIX_WIDTH - A_WIDTH
POOL_SIZES = (2, 4, 8, 16)
B_GROUPS = len(POOL_SIZES)
B_GW = B_WIDTH // B_GROUPS
POOL_CTX = max(POOL_SIZES) - 1
IN_A_WIDTH = 2 * A_HEADS * A_DK + 2 * A_WIDTH + B_WIDTH
C_HEADS = 16
C_KV_HEADS = 2
C_GROUP = C_HEADS // C_KV_HEADS
C_HD = 64
WINDOW = 128
C_SCALE = C_HD ** -0.5
IN_C_WIDTH = (C_HEADS + 2 * C_KV_HEADS) * C_HD
N_MEM = 256
M_HEADS = 4
M_HD = D_MODEL // M_HEADS
M_SCALE = M_HD ** -0.5
PEER_HEADS = 8
N_KEYS = 128
N_EXPERTS = N_KEYS * N_KEYS
PEER_DQ = 256
PEER_TOPK = 16
PEER_BLOCK = 128

kernel_name = 'hybrid_hgrn2_pool_swa_peer_step'

F32 = jnp.float32


def layer_norm(x, g, b):
    xf = x.astype(F32)
    mu = jnp.mean(xf, -1, keepdims=True)
    var = jnp.mean(jnp.square(xf - mu), -1, keepdims=True)
    return ((xf - mu) * lax.rsqrt(var + LN_EPS) * g.astype(F32) + b.astype(F32)).astype(x.dtype)


def post_norm(x, h, g, b):
    return layer_norm(ALPHA * x + h.astype(x.dtype), g, b)


def gla_chunked(q, k, v, log_f, s0):
    B, T, H, _ = q.shape
    DV = v.shape[-1]
    c = min(A_CHUNK, T)
    n = -(-T // c)
    tp = n * c

    def prep(a):
        a = jnp.pad(a.astype(F32), ((0, 0), (0, tp - T), (0, 0), (0, 0)))
        return jnp.moveaxis(a.reshape(B, n, c, H, a.shape[-1]), 1, 0)

    causal = jnp.tril(jnp.ones((c, c), bool))[None, :, :, None, None]

    def step(s, inp):
        qc, kc, vc, gc = inp
        b = jnp.cumsum(gc, axis=1)
        o_inter = jnp.einsum('bthk,bhkv->bthv', qc * jnp.exp(b), s)
        diff = b[:, :, None] - b[:, None, :]
        decay = jnp.where(causal, jnp.exp(jnp.minimum(diff, 0.0)), 0.0)
        att = jnp.einsum('bthk,bshk,btshk->bths', qc, kc, decay)
        o_intra = jnp.einsum('bths,bshv->bthv', att, vc)
        bl = b[:, -1]
        s_new = jnp.exp(bl)[..., None] * s + jnp.einsum('bshk,bshv->bhkv', kc * jnp.exp(bl[:, None] - b), vc)
        return s_new, o_inter + o_intra

    s_fin, o = lax.scan(step, s0.astype(F32), (prep(q), prep(k), prep(v), prep(log_f)))
    o = jnp.moveaxis(o, 0, 1).reshape(B, tp, H, DV)[:, :T]
    return o, s_fin


def pool_mixer(u, ctx, pos0, pool_w, pool_scale):
    B, T, W = u.shape
    ue = jnp.concatenate([ctx.astype(u.dtype), u], axis=1)
    cs = jnp.cumsum(ue.astype(F32), axis=1)
    cs = jnp.concatenate([jnp.zeros((B, 1, W), F32), cs], axis=1)
    pos = pos0 + jnp.arange(T)
    outs = []
    for gi, w in enumerate(POOL_SIZES):
        sl = slice(gi * B_GW, (gi + 1) * B_GW)
        hi = cs[:, POOL_CTX + 1:POOL_CTX + 1 + T, sl]
        lo = cs[:, POOL_CTX + 1 - w:POOL_CTX + 1 - w + T, sl]
        cnt = jnp.minimum(pos + 1, w).astype(F32)[None, :, None]
        outs.append((hi - lo) / cnt - u[:, :, sl].astype(F32))
    pooled = jnp.stack(outs, axis=2)
    y = jnp.einsum('btgc,gcd->btgd', pooled, pool_w.astype(F32)).reshape(B, T, W) * pool_scale.astype(F32)
    return y.astype(u.dtype), ue[:, -POOL_CTX:]


def hgrn_pool_mixer(x, a_idx, w_in, gamma, norm_g, pool_w, pool_scale, w_out, s0, pool_ctx, pos0):
    B, T, _ = x.shape
    proj = jnp.einsum('btd,de->bte', x, w_in)
    qk_w = A_HEADS * A_DK
    q, f, i, g, u = jnp.split(proj, [qk_w, 2 * qk_w, 2 * qk_w + A_WIDTH, 2 * qk_w + 2 * A_WIDTH], axis=-1)
    lb = jnp.cumsum(jax.nn.softmax(gamma.astype(F32), axis=0), axis=0)[a_idx]
    fg = lb + (1.0 - lb) * jax.nn.sigmoid(f.astype(F32))
    heads = lambda a, d: a.reshape(B, T, A_HEADS, d)
    o, s_new = gla_chunked(heads(jax.nn.silu(q.astype(F32)), A_DK), heads(1.0 - fg, A_DK),
                           heads(i, A_DV), heads(jnp.log(fg), A_DK), s0)
    o = o * lax.rsqrt(jnp.mean(jnp.square(o), -1, keepdims=True) + LN_EPS)
    o = (o.reshape(B, T, A_WIDTH) * norm_g.astype(F32) * jax.nn.silu(g.astype(F32))).astype(x.dtype)
    pooled, ctx_new = pool_mixer(u, pool_ctx, pos0, pool_w, pool_scale)
    out = jnp.einsum('bte,ed->btd', jnp.concatenate([o, pooled], axis=-1), w_out)
    return out, s_new.astype(s0.dtype), ctx_new


def sink_softmax(s, mask, sink):
    s = jnp.where(mask, s, -jnp.inf)
    m = jnp.maximum(jnp.max(s, -1, keepdims=True), sink)
    p = jnp.exp(s - m)
    return p / (jnp.sum(p, -1, keepdims=True) + jnp.exp(sink - m))


def swa_banded(q, k, v, sink):
    B, T = q.shape[:2]
    nb = -(-T // WINDOW)
    tp = nb * WINDOW
    padt = lambda a: jnp.pad(a, ((0, 0), (0, tp - T)) + ((0, 0),) * (a.ndim - 2))
    qb = padt(q).reshape(B, nb, WINDOW, C_KV_HEADS, C_GROUP, C_HD)
    kb = padt(k).reshape(B, nb, WINDOW, C_KV_HEADS, C_HD)
    vb = padt(v).reshape(B, nb, WINDOW, C_KV_HEADS, C_HD)
    prev = lambda a: jnp.pad(a, ((0, 0), (1, 0), (0, 0), (0, 0), (0, 0)))[:, :-1]
    kk = jnp.concatenate([prev(kb), kb], axis=2)
    vv = jnp.concatenate([prev(vb), vb], axis=2)
    s = jnp.einsum('bnqhgd,bnkhd->bnhgqk', qb, kk).astype(F32) * C_SCALE
    qi = jnp.arange(WINDOW)[:, None]
    kj = jnp.arange(2 * WINDOW)[None, :]
    band = (kj > qi) & (kj <= qi + WINDOW)
    first = (jnp.arange(nb)[:, None, None] > 0) | (kj >= WINDOW)[None]
    mask = (band[None] & first)[None, :, None, None]
    p = sink_softmax(s, mask, sink[None, None, :, :, None, None])
    o = jnp.einsum('bnhgqk,bnkhd->bnqhgd', p.astype(vv.dtype), vv)
    return o.reshape(B, tp, C_KV_HEADS, C_GROUP, C_HD)[:, :T]


def swa_with_buffer(q, kk, vv, sink):
    T = q.shape[1]
    s = jnp.einsum('bqhgd,bkhd->bhgqk', q, kk).astype(F32) * C_SCALE
    qi = jnp.arange(T)[:, None]
    kj = jnp.arange(WINDOW + T)[None, :]
    mask = ((kj > qi) & (kj <= qi + WINDOW))[None, None, None]
    p = sink_softmax(s, mask, sink[None, :, :, None, None])
    return jnp.einsum('bhgqk,bkhd->bqhgd', p.astype(vv.dtype), vv)


def swa_mixer(x, w_in, sinks, w_out, cache_k=None, cache_v=None):
    B, T, _ = x.shape
    proj = jnp.einsum('btd,de->bte', x, w_in)
    q, k, v = jnp.split(proj, [C_HEADS * C_HD, (C_HEADS + C_KV_HEADS) * C_HD], axis=-1)
    q = q.reshape(B, T, C_KV_HEADS, C_GROUP, C_HD)
    k = k.reshape(B, T, C_KV_HEADS, C_HD)
    v = v.reshape(B, T, C_KV_HEADS, C_HD)
    sink = sinks.astype(F32).reshape(C_KV_HEADS, C_GROUP)
    if cache_k is None:
        o = swa_banded(q, k, v, sink)
        new_k, new_v = k[:, -WINDOW:], v[:, -WINDOW:]
    else:
        kk = jnp.concatenate([cache_k.astype(k.dtype), k], axis=1)
        vv = jnp.concatenate([cache_v.astype(v.dtype), v], axis=1)
        o = swa_with_buffer(q, kk, vv, sink)
        new_k, new_v = kk[:, -WINDOW:], vv[:, -WINDOW:]
    out = jnp.einsum('bte,ed->btd', o.reshape(B, T, C_HEADS * C_HD), w_out)
    return out, new_k, new_v


def mem_kv(mem, w_kv):
    B, M, _ = mem.shape
    k, v = jnp.split(jnp.einsum('bmd,de->bme', mem, w_kv), 2, axis=-1)
    return k.reshape(B, M, M_HEADS, M_HD), v.reshape(B, M, M_HEADS, M_HD)


def cross_attn(x, mk, mv, w_q, w_o):
    B, T, _ = x.shape
    q = jnp.einsum('btd,de->bte', x, w_q).reshape(B, T, M_HEADS, M_HD)
    s = jnp.einsum('bthd,bmhd->bhtm', q, mk.astype(q.dtype)).astype(F32) * M_SCALE
    p = jax.nn.softmax(s, axis=-1)
    o = jnp.einsum('bhtm,bmhd->bthd', p.astype(q.dtype), mv.astype(q.dtype))
    return jnp.einsum('bte,ed->btd', o.reshape(B, T, M_HEADS * M_HD), w_o)


def peer_ffn(x, w_q, subkeys, u_tab, v_tab):
    B, T, D = x.shape
    xt = x.reshape(B * T, D)
    n = xt.shape[0]
    nb = -(-n // PEER_BLOCK)
    xt = jnp.pad(xt, ((0, nb * PEER_BLOCK - n), (0, 0))).reshape(nb, PEER_BLOCK, D)

    def one_block(xb):
        q = jnp.einsum('td,de->te', xb, w_q).reshape(PEER_BLOCK, PEER_HEADS, 2, PEER_DQ // 2)
        s = jnp.einsum('thpc,hpkc->thpk', q, subkeys).astype(F32)
        v1, i1 = lax.top_k(s[:, :, 0], PEER_TOPK)
        v2, i2 = lax.top_k(s[:, :, 1], PEER_TOPK)
        cand = (v1[..., :, None] + v2[..., None, :]).reshape(PEER_BLOCK, PEER_HEADS, PEER_TOPK * PEER_TOPK)
        cidx = (i1[..., :, None] * N_KEYS + i2[..., None, :]).reshape(PEER_BLOCK, PEER_HEADS, PEER_TOPK * PEER_TOPK)
        top_s, sel = lax.top_k(cand, PEER_TOPK)
        eidx = jnp.take_along_axis(cidx, sel, axis=-1).reshape(PEER_BLOCK, PEER_HEADS * PEER_TOPK)
        gates = jax.nn.softmax(top_s, axis=-1).reshape(PEER_BLOCK, PEER_HEADS * PEER_TOPK)
        h = jax.nn.gelu(jnp.einsum('td,ted->te', xb, u_tab[eidx]).astype(F32))
        w = (gates * h).astype(xb.dtype)
        return jnp.einsum('te,ted->td', w, v_tab[eidx])

    y = lax.map(one_block, xt).reshape(nb * PEER_BLOCK, D)[:n]
    return y.reshape(B, T, D)


def setup_inputs(seed: int = 0) -> dict:
    key = jax.random.key(seed)
    ks = jax.random.split(key, 40)
    nrm = lambda k, shape, scale: jax.random.normal(k, shape, F32) * scale
    d = D_MODEL
    w_in_c = jnp.concatenate([
        nrm(ks[13], (N_ODD, d, (C_HEADS + C_KV_HEADS) * C_HD), d ** -0.5),
        nrm(ks[14], (N_ODD, d, C_KV_HEADS * C_HD), d ** -0.5 * BETA)], axis=-1)
    mem_wkv = jnp.concatenate([
        nrm(ks[18], (DEPTH, d, M_HEADS * M_HD), d ** -0.5),
        nrm(ks[19], (DEPTH, d, M_HEADS * M_HD), d ** -0.5 * BETA)], axis=-1)
    return {
        'x_prompt': nrm(ks[0], (BATCH, SEQ, d), 1.0),
        'x_sample': nrm(ks[1], (DEC_BATCH, DEC_SEQ, d), 1.0),
        'mem_prompt': nrm(ks[2], (BATCH, N_MEM, d), 1.0),
        'state_hgrn': nrm(ks[3], (N_EVEN, DEC_BATCH, A_HEADS, A_DK, A_DV), 0.5),
        'state_pool': nrm(ks[4], (N_EVEN, DEC_BATCH, POOL_CTX, B_WIDTH), 1.0),
        'cache_swa_k': nrm(ks[5], (N_ODD, DEC_BATCH, WINDOW, C_KV_HEADS, C_HD), 1.0),
        'cache_swa_v': nrm(ks[6], (N_ODD, DEC_BATCH, WINDOW, C_KV_HEADS, C_HD), 1.0),
        'cache_mem_k': nrm(ks[7], (DEPTH, DEC_BATCH, N_MEM, M_HEADS, M_HD), 1.0),
        'cache_mem_v': nrm(ks[8], (DEPTH, DEC_BATCH, N_MEM, M_HEADS, M_HD), 0.5),
        'w_in_a': nrm(ks[9], (N_EVEN, d, IN_A_WIDTH), d ** -0.5),
        'hgrn_gamma': nrm(ks[10], (N_EVEN + 1, A_HEADS * A_DK), 0.5),
        'hgrn_norm_g': 1.0 + nrm(ks[11], (N_EVEN, A_WIDTH), 0.05),
        'pool_w': nrm(ks[12], (N_EVEN, B_GROUPS, B_GW, B_GW), B_GW ** -0.5),
        'pool_scale': 1.0 + nrm(ks[20], (N_EVEN, B_WIDTH), 0.05),
        'w_out_a': nrm(ks[21], (N_EVEN, MIX_WIDTH, d), MIX_WIDTH ** -0.5 * BETA),
        'w_in_c': w_in_c,
        'attn_sinks': nrm(ks[15], (N_ODD, C_HEADS), 1.0),
        'w_out_c': nrm(ks[16], (N_ODD, C_HEADS * C_HD, d), (C_HEADS * C_HD) ** -0.5 * BETA),
        'mem_wq': nrm(ks[17], (DEPTH, d, M_HEADS * M_HD), d ** -0.5),
        'mem_wkv': mem_wkv,
        'mem_wo': nrm(ks[22], (DEPTH, M_HEADS * M_HD, d), (M_HEADS * M_HD) ** -0.5 * BETA),
        'peer_wq': nrm(ks[23], (DEPTH, d, PEER_HEADS * PEER_DQ), d ** -0.5),
        'peer_subkeys': nrm(ks[24], (DEPTH, PEER_HEADS, 2, N_KEYS, PEER_DQ // 2), (PEER_DQ // 2) ** -0.5),
        'peer_u': nrm(ks[25], (DEPTH, N_EXPERTS, d), d ** -0.5),
        'peer_v': nrm(ks[26], (DEPTH, N_EXPERTS, d), BETA),
        'ln_g': 1.0 + nrm(ks[27], (DEPTH, 3, d), 0.05),
        'ln_b': nrm(ks[28], (DEPTH, 3, d), 0.02),
    }


def reference(x_prompt, x_sample, mem_prompt, state_hgrn, state_pool, cache_swa_k, cache_swa_v,
              cache_mem_k, cache_mem_v, w_in_a, hgrn_gamma, hgrn_norm_g, pool_w, pool_scale, w_out_a,
              w_in_c, attn_sinks, w_out_c, mem_wq, mem_wkv, mem_wo, peer_wq, peer_subkeys, peer_u,
              peer_v, ln_g, ln_b):
    yp, ys = x_prompt, x_sample
    bp = x_prompt.shape[0]
    hgrn_p, hgrn_s, pool_p, pool_s = [], [], [], []
    swa_kp, swa_ks, swa_vp, swa_vs = [], [], [], []
    mem_kp, mem_vp = [], []
    for l in range(DEPTH):
        if l % 2 == 0:
            a = l // 2
            s0 = jnp.zeros((bp, A_HEADS, A_DK, A_DV), yp.dtype)
            ctx0 = jnp.zeros((bp, POOL_CTX, B_WIDTH), yp.dtype)
            hp, sp_new, cp_new = hgrn_pool_mixer(yp, a, w_in_a[a], hgrn_gamma, hgrn_norm_g[a], pool_w[a],
                                                 pool_scale[a], w_out_a[a], s0, ctx0, 0)
            hs, ss_new, cs_new = hgrn_pool_mixer(ys, a, w_in_a[a], hgrn_gamma, hgrn_norm_g[a], pool_w[a],
                                                 pool_scale[a], w_out_a[a], state_hgrn[a], state_pool[a], PAST_LEN)
            hgrn_p.append(sp_new)
            hgrn_s.append(ss_new)
            pool_p.append(cp_new)
            pool_s.append(cs_new)
        else:
            c = l // 2
            hp, kp_new, vp_new = swa_mixer(yp, w_in_c[c], attn_sinks[c], w_out_c[c])
            hs, ks_new, vs_new = swa_mixer(ys, w_in_c[c], attn_sinks[c], w_out_c[c], cache_swa_k[c], cache_swa_v[c])
            swa_kp.append(kp_new)
            swa_ks.append(ks_new)
            swa_vp.append(vp_new)
            swa_vs.append(vs_new)
        yp = post_norm(yp, hp, ln_g[l, 0], ln_b[l, 0])
        ys = post_norm(ys, hs, ln_g[l, 0], ln_b[l, 0])
        mk, mv = mem_kv(mem_prompt, mem_wkv[l])
        mem_kp.append(mk)
        mem_vp.append(mv)
        yp = post_norm(yp, cross_attn(yp, mk, mv, mem_wq[l], mem_wo[l]), ln_g[l, 1], ln_b[l, 1])
        ys = post_norm(ys, cross_attn(ys, cache_mem_k[l], cache_mem_v[l], mem_wq[l], mem_wo[l]), ln_g[l, 1], ln_b[l, 1])
        yp = post_norm(yp, peer_ffn(yp, peer_wq[l], peer_subkeys[l], peer_u[l], peer_v[l]), ln_g[l, 2], ln_b[l, 2])
        ys = post_norm(ys, peer_ffn(ys, peer_wq[l], peer_subkeys[l], peer_u[l], peer_v[l]), ln_g[l, 2], ln_b[l, 2])
    return (yp, ys, jnp.stack(hgrn_p), jnp.stack(hgrn_s), jnp.stack(pool_p), jnp.stack(pool_s),
            jnp.stack(swa_kp), jnp.stack(swa_ks), jnp.stack(swa_vp), jnp.stack(swa_vs),
            jnp.stack(mem_kp), jnp.stack(mem_vp))
```

```python
import functools

import jax
import jax.numpy as jnp
from jax import lax
from jax.experimental import pallas as pl
from jax.experimental.pallas import tpu as pltpu

F32 = jnp.float32
BF16 = jnp.bfloat16

D_MODEL = 1024
DEPTH = 2
ALPHA = (2 * DEPTH) ** 0.25
LN_EPS = 1e-5
A_HEADS = 4
A_DK = 128
A_WIDTH = 512
A_CHUNK = 64
A_SUB = 16
B_WIDTH = 512
POOL_SIZES = (2, 4, 8, 16)
B_GW = 128
POOL_CTX = 15
IN_A_WIDTH = 2560
PAST_LEN = 16384
C_HEADS = 16
C_HD = 64
WINDOW = 128
C_SCALE = C_HD ** -0.5
N_MEM = 256
M_HEADS = 4
M_HD = 256
M_SCALE = M_HD ** -0.5
PEER_HEADS = 8
N_KEYS = 128
PEER_TOPK = 16
LANES = 128
VMEM_LIMIT = 48 * 1024 * 1024

NEG_INF = float("-inf")


def _cparams(*sem):
    return pltpu.CompilerParams(dimension_semantics=sem, vmem_limit_bytes=VMEM_LIMIT)


def _sigmoid(x):
    return 1.0 / (1.0 + jnp.exp(-x))


def _silu(x):
    return x * _sigmoid(x)


def _gelu_tanh(x):
    return 0.5 * x * (1.0 + jnp.tanh(0.7978845608028654 * (x + 0.044715 * (x * x * x))))


def _layer_norm(x, g, b):
    mu = jnp.mean(x, axis=-1, keepdims=True)
    xc = x - mu
    var = jnp.mean(xc * xc, axis=-1, keepdims=True)
    return xc * lax.rsqrt(var + LN_EPS) * g + b


def _dot(a, b):
    return jnp.dot(a, b, preferred_element_type=F32)


def _dot_nt(a, b):
    return lax.dot_general(a, b, (((1,), (1,)), ((), ())), preferred_element_type=F32)


def _dot_tn(a, b):
    return lax.dot_general(a, b, (((0,), (0,)), ((), ())), preferred_element_type=F32)


def _dot_exact(a, b):
    return jnp.dot(a, b, preferred_element_type=F32, precision=lax.Precision.HIGHEST)


def _mm_kernel(x_ref, w_ref, o_ref):
    o_ref[...] = _dot(x_ref[...].astype(BF16), w_ref[...])


def _mm(x, w, tm):
    m, k = x.shape
    n = w.shape[1]
    return pl.pallas_call(
        _mm_kernel,
        grid=(m // tm,),
        in_specs=[pl.BlockSpec((tm, k), lambda i: (i, 0)), pl.BlockSpec((k, n), lambda i: (0, 0))],
        out_specs=pl.BlockSpec((tm, n), lambda i: (i, 0)),
        out_shape=jax.ShapeDtypeStruct((m, n), F32),
        compiler_params=_cparams("parallel"),
        name="mm",
    )(x, w)


def _mm_ln_kernel(*refs, nparts):
    a_refs = refs[:nparts]
    w_refs = refs[nparts:2 * nparts]
    r_ref, g_ref, b_ref, o_ref = refs[2 * nparts:]
    acc = ALPHA * r_ref[...]
    for a_ref, w_ref in zip(a_refs, w_refs):
        acc = acc + _dot(a_ref[...].astype(BF16), w_ref[...])
    o_ref[...] = _layer_norm(acc, g_ref[...], b_ref[...])


def _mm_ln(parts, ws, resid, g, b, tm):
    m, n = resid.shape
    in_specs = [pl.BlockSpec((tm, a.shape[1]), lambda i: (i, 0)) for a in parts]
    in_specs += [pl.BlockSpec(w.shape, lambda i: (0, 0)) for w in ws]
    in_specs += [pl.BlockSpec((tm, n), lambda i: (i, 0)),
                 pl.BlockSpec((1, n), lambda i: (0, 0)), pl.BlockSpec((1, n), lambda i: (0, 0))]
    return pl.pallas_call(
        functools.partial(_mm_ln_kernel, nparts=len(parts)),
        grid=(m // tm,),
        in_specs=in_specs,
        out_specs=pl.BlockSpec((tm, n), lambda i: (i, 0)),
        out_shape=jax.ShapeDtypeStruct((m, n), F32),
        compiler_params=_cparams("parallel"),
        name="mm_ln",
    )(*parts, *ws, resid, g.reshape(1, n), b.reshape(1, n))


def _add_ln_kernel(x_ref, h_ref, g_ref, b_ref, o_ref):
    o_ref[...] = _layer_norm(ALPHA * x_ref[...] + h_ref[...], g_ref[...], b_ref[...])


def _add_ln(x, h, g, b, tm):
    m, n = x.shape
    row = pl.BlockSpec((tm, n), lambda i: (i, 0))
    vec = pl.BlockSpec((1, n), lambda i: (0, 0))
    return pl.pallas_call(
        _add_ln_kernel, grid=(m // tm,), in_specs=[row, row, vec, vec], out_specs=row,
        out_shape=jax.ShapeDtypeStruct((m, n), F32), compiler_params=_cparams("parallel"), name="add_ln",
    )(x, h, g.reshape(1, n), b.reshape(1, n))


def _forget_floor(gam, a_idx):
    e = jnp.exp(gam - jnp.max(gam, axis=0, keepdims=True))
    return jnp.sum(e[:a_idx + 1], axis=0, keepdims=True) / jnp.sum(e, axis=0, keepdims=True)


def _gla_prompt_kernel(q_ref, f_ref, i_ref, g_ref, gam_ref, ng_ref, o_ref, s_ref, st_ref, dg_ref, *, a_idx, n_chunks):
    c = A_CHUNK
    lb = _forget_floor(gam_ref[...], a_idx)
    st_ref[...] = jnp.zeros_like(st_ref)
    row = lax.broadcasted_iota(jnp.int32, (c, c), 0)
    col = lax.broadcasted_iota(jnp.int32, (c, c), 1)
    tri = (row >= col).astype(F32)
    rowc = lax.broadcasted_iota(jnp.int32, (c, A_DK), 0)
    srow = lax.broadcasted_iota(jnp.int32, (A_SUB, 1), 0)
    ng = ng_ref[...]

    def chunk(ci, carry):
        r0 = pl.multiple_of(ci * c, c)
        q = q_ref[0, pl.ds(r0, c), :]
        f = f_ref[0, pl.ds(r0, c), :]
        v = i_ref[0, pl.ds(r0, c), :]
        gate = g_ref[0, pl.ds(r0, c), :]
        qs = _silu(q)
        fg = lb + (1.0 - lb) * _sigmoid(f)
        kk = 1.0 - fg
        b = _dot_exact(tri, jnp.log(fg))
        st = st_ref[...]
        o = _dot_nt((qs * jnp.exp(b)).astype(BF16), st.astype(BF16))
        vb = v.astype(BF16)
        n_sub = c // A_SUB
        for j in range(n_sub - 1):
            lo, hi = j * A_SUB, (j + 1) * A_SUB
            mj = b[hi - 1:hi]
            qr = jnp.where(rowc >= hi, qs * jnp.exp(jnp.minimum(b - mj, 0.0)), 0.0)
            kr = kk[lo:hi] * jnp.exp(mj - b[lo:hi])
            att = _dot_nt(qr.astype(BF16), kr.astype(BF16))
            o = o + _dot(att.astype(BF16), vb[lo:hi])
        for j in range(n_sub):
            lo, hi = j * A_SUB, (j + 1) * A_SUB
            kj, bj, vj = kk[lo:hi], b[lo:hi], v[lo:hi]
            for t in range(A_SUB):
                w = jnp.exp(jnp.minimum(b[lo + t:lo + t + 1] - bj, 0.0))
                a = jnp.sum(qs[lo + t:lo + t + 1] * kj * w, axis=1, keepdims=True)
                a = jnp.where(srow <= t, a, 0.0)
                dg_ref[lo + t:lo + t + 1, :] = jnp.sum(a * vj, axis=0, keepdims=True)
        o = o + dg_ref[...]
        bl = b[c - 1:c]
        kd = kk * jnp.exp(bl - b)
        st_ref[...] = st * jnp.exp(bl) + _dot_tn(vb, kd.astype(BF16))
        o = o * lax.rsqrt(jnp.mean(o * o, axis=-1, keepdims=True) + LN_EPS)
        o_ref[0, pl.ds(r0, c), :] = o * ng * _silu(gate)
        return carry

    lax.fori_loop(0, n_chunks, chunk, 0)
    s_ref[0, 0] = st_ref[...].T


def _gla_prompt(proj, gamma, norm_g, a_idx):
    bsz, t, _ = proj.shape
    blk = lambda off: pl.BlockSpec((1, t, A_DK), lambda b, h: (b, 0, off + h))
    return pl.pallas_call(
        functools.partial(_gla_prompt_kernel, a_idx=a_idx, n_chunks=t // A_CHUNK),
        grid=(bsz, A_HEADS),
        in_specs=[blk(0), blk(A_HEADS), blk(2 * A_HEADS), blk(3 * A_HEADS),
                  pl.BlockSpec((gamma.shape[0], A_DK), lambda b, h: (0, h)),
                  pl.BlockSpec((1, A_DK), lambda b, h: (0, h))],
        out_specs=[pl.BlockSpec((1, t, A_DK), lambda b, h: (b, 0, h)),
                   pl.BlockSpec((1, 1, A_DK, A_DK), lambda b, h: (b, h, 0, 0))],
        out_shape=[jax.ShapeDtypeStruct((bsz, t, A_WIDTH), F32),
                   jax.ShapeDtypeStruct((bsz, A_HEADS, A_DK, A_DK), F32)],
        scratch_shapes=[pltpu.VMEM((A_DK, A_DK), F32), pltpu.VMEM((A_CHUNK, A_DK), F32)],
        compiler_params=_cparams("parallel", "parallel"),
        name="gla_prompt",
    )(proj, proj, proj, proj, gamma, norm_g.reshape(1, A_WIDTH))


def _col(row):
    return jnp.broadcast_to(row, (LANES, LANES)).T


def _gla_sample_kernel(p_ref, s0_ref, gam_ref, ng_ref, o_ref, s_ref, *, a_idx, sb):
    def one(bi, carry):
        prow = p_ref[pl.ds(bi, 1), :]
        outs = []
        for h in range(A_HEADS):
            sl = pl.ds(h * A_DK, A_DK)
            blk = lambda i: prow[:, (i * A_HEADS + h) * A_DK:(i * A_HEADS + h + 1) * A_DK]
            q, f, v, gate = blk(0), blk(1), blk(2), blk(3)
            lb = _forget_floor(gam_ref[:, sl], a_idx)
            qs = _silu(q)
            fg = lb + (1.0 - lb) * _sigmoid(f)
            s_new = _col(fg) * s0_ref[bi, h] + _col(1.0 - fg) * v
            s_ref[bi, h] = s_new
            o = jnp.sum(_col(qs) * s_new, axis=0, keepdims=True)
            o = o * lax.rsqrt(jnp.mean(o * o, axis=-1, keepdims=True) + LN_EPS)
            outs.append(o * ng_ref[:, sl] * _silu(gate))
        o_ref[pl.ds(bi, 1), :] = jnp.concatenate(outs, axis=1)
        return carry

    lax.fori_loop(0, sb, one, 0)


def _gla_sample(proj, s0, gamma, norm_g, a_idx, sb=8):
    n = proj.shape[0]
    st = pl.BlockSpec((sb, A_HEADS, A_DK, A_DK), lambda i: (i, 0, 0, 0))
    return pl.pallas_call(
        functools.partial(_gla_sample_kernel, a_idx=a_idx, sb=sb),
        grid=(n // sb,),
        in_specs=[pl.BlockSpec((sb, IN_A_WIDTH), lambda i: (i, 0)), st,
                  pl.BlockSpec(gamma.shape, lambda i: (0, 0)), pl.BlockSpec((1, A_WIDTH), lambda i: (0, 0))],
        out_specs=[pl.BlockSpec((sb, A_WIDTH), lambda i: (i, 0)), st],
        out_shape=[jax.ShapeDtypeStruct((n, A_WIDTH), F32), jax.ShapeDtypeStruct(s0.shape, F32)],
        compiler_params=_cparams("parallel"),
        name="gla_sample",
    )(proj, s0, gamma, norm_g.reshape(1, A_WIDTH))


POOL_PAD = 16
POOL_TILE = 256


def _pool_prompt_kernel(u_ref, w_ref, sc_ref, o_ref, pad_ref, *, t):
    pad_ref[0:POOL_PAD, :] = jnp.zeros((POOL_PAD, B_WIDTH), F32)
    pad_ref[POOL_PAD:POOL_PAD + t, :] = u_ref[0]
    pos = lax.broadcasted_iota(jnp.int32, (POOL_TILE, B_GW), 0)

    def tile(ri, carry):
        r0 = pl.multiple_of(ri * POOL_TILE, POOL_TILE)
        outs = []
        for gi, w in enumerate(POOL_SIZES):
            x = pad_ref[pl.ds(r0, POOL_TILE + POOL_PAD), gi * B_GW:(gi + 1) * B_GW]
            u = x[POOL_PAD:]
            s = x
            sh = 1
            while sh < w:
                s = s[sh:] + s[:-sh]
                sh *= 2
            s = s[POOL_PAD - (w - 1):]
            cnt = jnp.minimum(pos + (r0 + 1), w).astype(F32)
            pooled = s / cnt - u
            outs.append(_dot(pooled.astype(BF16), w_ref[gi]))
        o_ref[0, pl.ds(r0, POOL_TILE), :] = jnp.concatenate(outs, axis=1) * sc_ref[...]
        return carry

    lax.fori_loop(0, t // POOL_TILE, tile, 0)


def _pool_prompt(proj, pool_w, pool_scale):
    bsz, t, _ = proj.shape
    return pl.pallas_call(
        functools.partial(_pool_prompt_kernel, t=t),
        grid=(bsz,),
        in_specs=[pl.BlockSpec((1, t, B_WIDTH), lambda b: (b, 0, (IN_A_WIDTH - B_WIDTH) // B_WIDTH)),
                  pl.BlockSpec(pool_w.shape, lambda b: (0, 0, 0)),
                  pl.BlockSpec((1, B_WIDTH), lambda b: (0, 0))],
        out_specs=pl.BlockSpec((1, t, B_WIDTH), lambda b: (b, 0, 0)),
        out_shape=jax.ShapeDtypeStruct((bsz, t, B_WIDTH), F32),
        scratch_shapes=[pltpu.VMEM((t + POOL_PAD, B_WIDTH), F32)],
        compiler_params=_cparams("parallel"),
        name="pool_prompt",
    )(proj, pool_w, pool_scale.reshape(1, B_WIDTH))


def _pool_sample_kernel(u_ref, ctx_ref, w_ref, sc_ref, o_ref):
    outs = []
    for gi, w in enumerate(POOL_SIZES):
        u = u_ref[:, gi * B_GW:(gi + 1) * B_GW]
        s = u
        for j in range(1, w):
            off = (POOL_CTX - j) * B_WIDTH + gi * B_GW
            s = s + ctx_ref[:, off:off + B_GW]
        pooled = s / float(min(PAST_LEN + 1, w)) - u
        outs.append(_dot(pooled.astype(BF16), w_ref[gi]))
    o_ref[...] = jnp.concatenate(outs, axis=1) * sc_ref[...]


def _pool_sample(u, ctx, pool_w, pool_scale):
    n = u.shape[0]
    full = lambda a: pl.BlockSpec(a.shape, lambda: (0,) * a.ndim)
    sc = pool_scale.reshape(1, B_WIDTH)
    return pl.pallas_call(
        _pool_sample_kernel, in_specs=[full(u), full(ctx), full(pool_w), full(sc)],
        out_specs=pl.BlockSpec((n, B_WIDTH), lambda: (0, 0)),
        out_shape=jax.ShapeDtypeStruct((n, B_WIDTH), F32),
        compiler_params=pltpu.CompilerParams(vmem_limit_bytes=VMEM_LIMIT), name="pool_sample",
    )(u, ctx, pool_w, sc)


def _dup_kv_halves(x):
    lane = lax.broadcasted_iota(jnp.int32, x.shape, 1)
    xr = pltpu.roll(x, C_HD, 1)
    return jnp.where(lane < C_HD, x, xr), jnp.where(lane < C_HD, xr, x)


def _swa_prompt_kernel(sink_ref, q_ref, kvp_ref, kvc_ref, o_ref):
    n = pl.program_id(1)
    k2 = jnp.concatenate([kvp_ref[0, :, :LANES], kvc_ref[0, :, :LANES]], axis=0)
    v2 = jnp.concatenate([kvp_ref[0, :, LANES:], kvc_ref[0, :, LANES:]], axis=0)
    kks = [x.astype(BF16) for x in _dup_kv_halves(k2)]
    vvs = [x.astype(BF16) for x in _dup_kv_halves(v2)]
    qi = lax.broadcasted_iota(jnp.int32, (WINDOW, 2 * WINDOW), 0)
    kj = lax.broadcasted_iota(jnp.int32, (WINDOW, 2 * WINDOW), 1)
    first_key = jnp.where(n > 0, 0, WINDOW)
    mask = (kj > qi) & (kj <= qi + WINDOW) & (kj >= first_key)
    lane = lax.broadcasted_iota(jnp.int32, (WINDOW, LANES), 1)
    heads_per_tile = LANES // C_HD
    for j in range(C_HEADS // heads_per_tile):
        kv = (j * heads_per_tile) // (C_HEADS // 2)
        qt = q_ref[0, :, j * LANES:(j + 1) * LANES]
        halves = []
        for par in range(heads_per_tile):
            sink = sink_ref[j * heads_per_tile + par]
            qm = jnp.where((lane >= par * C_HD) & (lane < (par + 1) * C_HD), qt, 0.0).astype(BF16)
            s = _dot_nt(qm, kks[kv]) * C_SCALE
            s = jnp.where(mask, s, NEG_INF)
            m = jnp.maximum(jnp.max(s, axis=-1, keepdims=True), sink)
            p = jnp.exp(s - m)
            p = p / (jnp.sum(p, axis=-1, keepdims=True) + jnp.exp(sink - m))
            halves.append(_dot(p.astype(BF16), vvs[kv]))
        o_ref[0, :, j * LANES:(j + 1) * LANES] = jnp.where(lane < C_HD, halves[0], halves[1])


def _swa_prompt(proj, sinks):
    bsz, t, width = proj.shape
    qw = C_HEADS * C_HD
    kvw = width - qw
    return pl.pallas_call(
        _swa_prompt_kernel,
        grid=(bsz, t // WINDOW),
        in_specs=[pl.BlockSpec(memory_space=pltpu.SMEM),
                  pl.BlockSpec((1, WINDOW, qw), lambda b, n: (b, n, 0)),
                  pl.BlockSpec((1, WINDOW, kvw), lambda b, n: (b, jnp.maximum(n - 1, 0), qw // kvw)),
                  pl.BlockSpec((1, WINDOW, kvw), lambda b, n: (b, n, qw // kvw))],
        out_specs=pl.BlockSpec((1, WINDOW, qw), lambda b, n: (b, n, 0)),
        out_shape=jax.ShapeDtypeStruct((bsz, t, qw), F32),
        compiler_params=_cparams("parallel", "parallel"),
        name="swa_prompt",
    )(sinks, proj, proj, proj)


def _swa_sample_kernel(sink_ref, p_ref, ck_ref, cv_ref, o_ref, *, sb):
    qw = C_HEADS * C_HD
    rowi = lax.broadcasted_iota(jnp.int32, (WINDOW, 1), 0)
    lane = lax.broadcasted_iota(jnp.int32, (1, LANES), 1)
    lo_half = lane < C_HD
    heads_per_tile = LANES // C_HD

    def one(bi, carry):
        prow = p_ref[pl.ds(bi, 1), :]
        knew = prow[:, qw:qw + LANES]
        vnew = prow[:, qw + LANES:qw + 2 * LANES]
        kc, vc = ck_ref[bi], cv_ref[bi]
        kcs, vcs = _dup_kv_halves(kc), _dup_kv_halves(vc)
        kns, vns = _dup_kv_halves(knew), _dup_kv_halves(vnew)
        tiles = []
        for j in range(C_HEADS // heads_per_tile):
            kv = (j * heads_per_tile) // (C_HEADS // 2)
            qt = prow[:, j * LANES:(j + 1) * LANES]
            pc = kcs[kv] * qt
            pn = kns[kv] * qt
            outs = []
            for par in range(heads_per_tile):
                sink = sink_ref[j * heads_per_tile + par]
                half = lo_half if par == 0 else jnp.logical_not(lo_half)
                sc = jnp.sum(jnp.where(half, pc, 0.0), axis=1, keepdims=True) * C_SCALE
                sn = jnp.sum(jnp.where(half, pn, 0.0), axis=1, keepdims=True) * C_SCALE
                sc = jnp.where(rowi >= 1, sc, NEG_INF)
                m = jnp.maximum(jnp.maximum(jnp.max(sc, axis=0, keepdims=True), sn), sink)
                ec = jnp.exp(sc - m)
                en = jnp.exp(sn - m)
                den = jnp.sum(ec, axis=0, keepdims=True) + en + jnp.exp(sink - m)
                outs.append((jnp.sum(ec * vcs[kv], axis=0, keepdims=True) + en * vns[kv]) / den)
            tiles.append(jnp.where(lo_half, outs[0], outs[1]))
        o_ref[pl.ds(bi, 1), :] = jnp.concatenate(tiles, axis=1)
        return carry

    lax.fori_loop(0, sb, one, 0)


def _swa_sample(proj, cache_k, cache_v, sinks, sb=8):
    n, width = proj.shape
    qw = C_HEADS * C_HD
    cache = pl.BlockSpec((sb, WINDOW, LANES), lambda i: (i, 0, 0))
    return pl.pallas_call(
        functools.partial(_swa_sample_kernel, sb=sb),
        grid=(n // sb,),
        in_specs=[pl.BlockSpec(memory_space=pltpu.SMEM), pl.BlockSpec((sb, width), lambda i: (i, 0)), cache, cache],
        out_specs=pl.BlockSpec((sb, qw), lambda i: (i, 0)),
        out_shape=jax.ShapeDtypeStruct((n, qw), F32),
        compiler_params=_cparams("parallel"),
        name="swa_sample",
    )(sinks, proj, cache_k, cache_v)


def _cross_prompt_kernel(x_ref, k_ref, v_ref, wq_ref, wo_ref, g_ref, b_ref, o_ref):
    x = x_ref[0]
    q = _dot(x.astype(BF16), wq_ref[...])
    outs = []
    for h in range(M_HEADS):
        sl = slice(h * M_HD, (h + 1) * M_HD)
        s = _dot_nt(q[:, sl].astype(BF16), k_ref[0, :, sl].astype(BF16)) * M_SCALE
        p = jnp.exp(s - jnp.max(s, axis=-1, keepdims=True))
        p = p / jnp.sum(p, axis=-1, keepdims=True)
        outs.append(_dot(p.astype(BF16), v_ref[0, :, sl].astype(BF16)))
    o = jnp.concatenate(outs, axis=1)
    o_ref[0] = _layer_norm(ALPHA * x + _dot(o.astype(BF16), wo_ref[...]), g_ref[...], b_ref[...])


def _cross_prompt(x, mk, mv, wq, wo, g, b, tq=256):
    bsz, t, d = x.shape
    kvs = pl.BlockSpec((1, N_MEM, d), lambda bi, i: (bi, 0, 0))
    wsp = pl.BlockSpec((d, d), lambda bi, i: (0, 0))
    vec = pl.BlockSpec((1, d), lambda bi, i: (0, 0))
    xs = pl.BlockSpec((1, tq, d), lambda bi, i: (bi, i, 0))
    return pl.pallas_call(
        _cross_prompt_kernel, grid=(bsz, t // tq),
        in_specs=[xs, kvs, kvs, wsp, wsp, vec, vec], out_specs=xs,
        out_shape=jax.ShapeDtypeStruct(x.shape, F32),
        compiler_params=_cparams("parallel", "parallel"), name="cross_prompt",
    )(x, mk, mv, wq, wo, g.reshape(1, d), b.reshape(1, d))


def _cross_sample_kernel(q_ref, k_ref, v_ref, o_ref, *, sb):
    base = pl.program_id(0) * sb

    def one(bi, carry):
        for h in range(M_HEADS):
            sl = pl.ds(h * M_HD, M_HD)
            q = q_ref[pl.ds(base + bi, 1), sl]
            s = jnp.sum(k_ref[bi, :, sl] * q, axis=1, keepdims=True) * M_SCALE
            p = jnp.exp(s - jnp.max(s, axis=0, keepdims=True))
            den = jnp.sum(p, axis=0, keepdims=True)
            o_ref[pl.ds(base + bi, 1), sl] = jnp.sum(p * v_ref[bi, :, sl], axis=0, keepdims=True) / den
        return carry

    lax.fori_loop(0, sb, one, 0)


def _cross_sample(q, ck, cv, sb=4):
    n, d = q.shape
    cache = pl.BlockSpec((sb, N_MEM, d), lambda i: (i, 0, 0))
    rows = pl.BlockSpec((n, d), lambda i: (0, 0))
    return pl.pallas_call(
        functools.partial(_cross_sample_kernel, sb=sb), grid=(n // sb,),
        in_specs=[rows, cache, cache], out_specs=rows,
        out_shape=jax.ShapeDtypeStruct((n, d), F32),
        compiler_params=_cparams("arbitrary"), name="cross_sample",
    )(q, ck, cv)


def _peer_scores_kernel(x_ref, wq_ref, sk_ref, o_ref):
    q = _dot(x_ref[...].astype(BF16), wq_ref[...]).astype(BF16)
    for hp in range(2 * PEER_HEADS):
        o_ref[hp] = _dot_nt(sk_ref[hp], q[:, hp * N_KEYS:(hp + 1) * N_KEYS])


def _peer_scores(x, wq, sk, tm):
    n, d = x.shape
    hp = 2 * PEER_HEADS
    return pl.pallas_call(
        _peer_scores_kernel, grid=(n // tm,),
        in_specs=[pl.BlockSpec((tm, d), lambda i: (i, 0)), pl.BlockSpec(wq.shape, lambda i: (0, 0)),
                  pl.BlockSpec(sk.shape, lambda i: (0, 0, 0))],
        out_specs=pl.BlockSpec((hp, N_KEYS, tm), lambda i: (0, 0, i)),
        out_shape=jax.ShapeDtypeStruct((hp, N_KEYS, n), F32),
        compiler_params=_cparams("parallel"), name="peer_scores",
    )(x, wq, sk)


def _staircase():
    return [(a, b) for a in range(PEER_TOPK) for b in range(PEER_TOPK) if (a + 1) * (b + 1) <= PEER_TOPK]


N_CAND = 56


def _top16(s):
    key = lax.broadcasted_iota(jnp.int32, s.shape, 0)
    r16 = lax.broadcasted_iota(jnp.int32, (PEER_TOPK, s.shape[1]), 0)
    rank = jnp.full(s.shape, PEER_TOPK, jnp.int32)
    vals = jnp.zeros((PEER_TOPK, s.shape[1]), F32)
    for a in range(PEER_TOPK):
        m = jnp.max(s, axis=0, keepdims=True)
        idx = jnp.min(jnp.where(s == m, key, s.shape[0]), axis=0, keepdims=True)
        sel = key == idx
        rank = jnp.where(sel, a, rank)
        s = jnp.where(sel, NEG_INF, s)
        vals = jnp.where(r16 == a, m, vals)
    return vals, rank


def _peer_topk_kernel(s_ref, a1_ref, a2_ref, a1t_ref, e1_ref, m_ref, e2_ref, r2_ref):
    s1, s2 = s_ref[0], s_ref[1]
    lanes = s1.shape[1]
    v1, rank1 = _top16(s1)
    v2, rank2 = _top16(s2)
    cand = _dot_exact(a1_ref[...], v1) + _dot_exact(a2_ref[...], v2)
    crow = lax.broadcasted_iota(jnp.int32, (N_CAND, lanes), 0)
    cand = jnp.where(crow < len(_staircase()), cand, NEG_INF)
    taken = jnp.zeros((N_CAND, lanes), F32)
    z = jnp.zeros((1, lanes), F32)
    top0 = None
    for j in range(PEER_TOPK):
        m = jnp.max(cand, axis=0, keepdims=True)
        idx = jnp.min(jnp.where(cand == m, crow, N_CAND), axis=0, keepdims=True)
        sel = crow == idx
        taken = jnp.where(sel, 1.0, taken)
        cand = jnp.where(sel, NEG_INF, cand)
        if j == 0:
            top0 = m
        z = z + jnp.exp(m - top0)
    count = _dot(a1t_ref[...], taken.astype(BF16))
    mk = jnp.zeros(s1.shape, F32)
    for a in range(PEER_TOPK):
        mk = jnp.where(rank1 == a, count[a:a + 1], mk)
    e1_ref[0] = jnp.where(rank1 < PEER_TOPK, jnp.exp(s1 - v1[0:1]), 0.0)
    m_ref[0] = mk
    e2_ref[0] = jnp.where(rank2 < PEER_TOPK, jnp.exp(s2 - v2[0:1]) / z, 0.0)
    r2_ref[0] = rank2.astype(F32)


def _peer_topk(st, tl):
    hp, keys, n = st.shape
    stair = _staircase()
    a1 = jnp.array([[1.0 if (r < len(stair) and stair[r][0] == a) else 0.0 for a in range(PEER_TOPK)]
                    for r in range(N_CAND)], F32)
    a2 = jnp.array([[1.0 if (r < len(stair) and stair[r][1] == b) else 0.0 for b in range(PEER_TOPK)]
                    for r in range(N_CAND)], F32)
    out = pl.BlockSpec((1, keys, tl), lambda h, i: (h, 0, i))
    sel = pl.BlockSpec((N_CAND, PEER_TOPK), lambda h, i: (0, 0))
    selt = pl.BlockSpec((PEER_TOPK, N_CAND), lambda h, i: (0, 0))
    shp = jax.ShapeDtypeStruct((PEER_HEADS, keys, n), F32)
    return pl.pallas_call(
        _peer_topk_kernel, grid=(PEER_HEADS, n // tl),
        in_specs=[pl.BlockSpec((2, keys, tl), lambda h, i: (h, 0, i)), sel, sel, selt],
        out_specs=[out, out, out, out], out_shape=[shp, shp, shp, shp],
        compiler_params=_cparams("parallel", "parallel"), name="peer_topk",
    )(st, a1, a2, a1.T.astype(BF16))


def _peer_dense_kernel(x_ref, u_ref, vt_ref, e1_ref, m_ref, e2_ref, r2_ref, y_ref, acc_ref, h_ref, wm_ref, *, et, tl):
    j = pl.program_id(1)

    @pl.when(j == 0)
    def _():
        acc_ref[...] = jnp.zeros_like(acc_ref)

    h_ref[...] = _dot_nt(u_ref[...], x_ref[...].astype(BF16))
    for r in range(et // N_KEYS):
        i1 = j * (et // N_KEYS) + r
        e1s = [e1_ref[hd, pl.ds(i1, 1), :] for hd in range(PEER_HEADS)]
        mks = [m_ref[hd, pl.ds(i1, 1), :] for hd in range(PEER_HEADS)]
        for c in range(tl // LANES):
            cs = slice(c * LANES, (c + 1) * LANES)
            g = jnp.zeros((N_KEYS, LANES), F32)
            for hd in range(PEER_HEADS):
                g = g + e1s[hd][:, cs] * jnp.where(r2_ref[hd, :, cs] < mks[hd][:, cs], e2_ref[hd, :, cs], 0.0)
            hb = h_ref[r * N_KEYS:(r + 1) * N_KEYS, cs]
            wm_ref[r * N_KEYS:(r + 1) * N_KEYS, cs] = (g * _gelu_tanh(hb)).astype(BF16)
    acc_ref[...] += _dot(vt_ref[...], wm_ref[...])

    @pl.when(j == pl.num_programs(1) - 1)
    def _():
        y_ref[...] = acc_ref[...].T


def _peer_dense(x, u, vt, e1, mk, e2, r2, tl, et):
    n, d = x.shape
    ne = u.shape[0]
    hlp = pl.BlockSpec((PEER_HEADS, N_KEYS, tl), lambda i, j: (0, 0, i))
    return pl.pallas_call(
        functools.partial(_peer_dense_kernel, et=et, tl=tl),
        grid=(n // tl, ne // et),
        in_specs=[pl.BlockSpec((tl, d), lambda i, j: (i, 0)),
                  pl.BlockSpec((et, d), lambda i, j: (j, 0)),
                  pl.BlockSpec((d, et), lambda i, j: (0, j)),
                  hlp, hlp, hlp, hlp],
        out_specs=pl.BlockSpec((tl, d), lambda i, j: (i, 0)),
        out_shape=jax.ShapeDtypeStruct((n, d), F32),
        scratch_shapes=[pltpu.VMEM((d, tl), F32), pltpu.VMEM((et, tl), F32), pltpu.VMEM((et, tl), BF16)],
        compiler_params=_cparams("parallel", "arbitrary"), name="peer_dense",
    )(x, u, vt, e1, mk, e2, r2)


def _peer(x, wq, sk, u, vt, g, b, tl, et, tm):
    st = _peer_scores(x, wq, sk, tm)
    e1, mk, e2, r2 = _peer_topk(st, min(tl, 256))
    y = _peer_dense(x, u, vt, e1, mk, e2, r2, tl, et)
    return _add_ln(x, y, g, b, tm)


def kernel(x_prompt, x_sample, mem_prompt, state_hgrn, state_pool, cache_swa_k, cache_swa_v, cache_mem_k, cache_mem_v, w_in_a, hgrn_gamma, hgrn_norm_g, pool_w, pool_scale, w_out_a, w_in_c, attn_sinks, w_out_c, mem_wq, mem_wkv, mem_wo, peer_wq, peer_subkeys, peer_u, peer_v, ln_g, ln_b):
    bp, t, d = x_prompt.shape
    ns = x_sample.shape[0]
    n = bp * t
    bf = lambda w: w.astype(BF16)
    yp = x_prompt.reshape(n, d)
    ys = x_sample.reshape(ns, d)
    mem = mem_prompt.reshape(bp * N_MEM, d)
    tm = 512

    hgrn_p, hgrn_s, pool_p, pool_s = [], [], [], []
    swa_kp, swa_ks, swa_vp, swa_vs = [], [], [], []
    mem_kp, mem_vp = [], []
    for l in range(DEPTH):
        if l % 2 == 0:
            a = l // 2
            w_in, w_out = bf(w_in_a[a]), bf(w_out_a[a])
            pw = bf(pool_w[a])
            proj = _mm(yp, w_in, tm).reshape(bp, t, IN_A_WIDTH)
            o, s_new = _gla_prompt(proj, hgrn_gamma, hgrn_norm_g[a], a)
            pooled = _pool_prompt(proj, pw, pool_scale[a])
            hgrn_p.append(s_new)
            pool_p.append(proj[:, t - POOL_CTX:, IN_A_WIDTH - B_WIDTH:])
            yp = _mm_ln([o.reshape(n, A_WIDTH), pooled.reshape(n, B_WIDTH)], [w_out[:A_WIDTH], w_out[A_WIDTH:]],
                        yp, ln_g[l, 0], ln_b[l, 0], tm)
            proj_s = _mm(ys, w_in, ns)
            o_s, s_new_s = _gla_sample(proj_s, state_hgrn[a], hgrn_gamma, hgrn_norm_g[a], a)
            u_s = proj_s[:, IN_A_WIDTH - B_WIDTH:]
            pooled_s = _pool_sample(u_s, state_pool[a].reshape(ns, POOL_CTX * B_WIDTH), pw, pool_scale[a])
            hgrn_s.append(s_new_s)
            pool_s.append(jnp.concatenate([state_pool[a][:, 1:], u_s[:, None]], axis=1))
            ys = _mm_ln([o_s, pooled_s], [w_out[:A_WIDTH], w_out[A_WIDTH:]], ys, ln_g[l, 0], ln_b[l, 0], ns)
        else:
            c = l // 2
            w_in, w_out = bf(w_in_c[c]), bf(w_out_c[c])
            qw = C_HEADS * C_HD
            proj = _mm(yp, w_in, tm).reshape(bp, t, -1)
            o = _swa_prompt(proj, attn_sinks[c])
            swa_kp.append(proj[:, t - WINDOW:, qw:qw + LANES].reshape(bp, WINDOW, 2, C_HD))
            swa_vp.append(proj[:, t - WINDOW:, qw + LANES:].reshape(bp, WINDOW, 2, C_HD))
            yp = _mm_ln([o.reshape(n, qw)], [w_out], yp, ln_g[l, 0], ln_b[l, 0], tm)
            proj_s = _mm(ys, w_in, ns)
            ck = cache_swa_k[c].reshape(ns, WINDOW, LANES)
            cv = cache_swa_v[c].reshape(ns, WINDOW, LANES)
            o_s = _swa_sample(proj_s, ck, cv, attn_sinks[c])
            swa_ks.append(jnp.concatenate([ck[:, 1:], proj_s[:, None, qw:qw + LANES]], axis=1).reshape(ns, WINDOW, 2, C_HD))
            swa_vs.append(jnp.concatenate([cv[:, 1:], proj_s[:, None, qw + LANES:]], axis=1).reshape(ns, WINDOW, 2, C_HD))
            ys = _mm_ln([o_s], [w_out], ys, ln_g[l, 0], ln_b[l, 0], ns)

        wq, wo = bf(mem_wq[l]), bf(mem_wo[l])
        kv = _mm(mem, bf(mem_wkv[l]), tm)
        mk = kv[:, :d].reshape(bp, N_MEM, d)
        mv = kv[:, d:].reshape(bp, N_MEM, d)
        mem_kp.append(mk.reshape(bp, N_MEM, M_HEADS, M_HD))
        mem_vp.append(mv.reshape(bp, N_MEM, M_HEADS, M_HD))
        yp = _cross_prompt(yp.reshape(bp, t, d), mk, mv, wq, wo, ln_g[l, 1], ln_b[l, 1]).reshape(n, d)
        q_s = _mm(ys, wq, ns)
        o_s = _cross_sample(q_s, cache_mem_k[l].reshape(ns, N_MEM, d), cache_mem_v[l].reshape(ns, N_MEM, d))
        ys = _mm_ln([o_s], [wo], ys, ln_g[l, 1], ln_b[l, 1], ns)

        pq = bf(peer_wq[l])
        sk = bf(peer_subkeys[l].reshape(2 * PEER_HEADS, N_KEYS, -1))
        u = bf(peer_u[l])
        vt = bf(peer_v[l]).T
        yp = _peer(yp, pq, sk, u, vt, ln_g[l, 2], ln_b[l, 2], tl=512, et=512, tm=tm)
        ys = _peer(ys, pq, sk, u, vt, ln_g[l, 2], ln_b[l, 2], tl=ns, et=512, tm=ns)

    return (yp.reshape(bp, t, d), ys.reshape(ns, 1, d), jnp.stack(hgrn_p), jnp.stack(hgrn_s), jnp.stack(pool_p),
            jnp.stack(pool_s), jnp.stack(swa_kp), jnp.stack(swa_ks), jnp.stack(swa_vp), jnp.stack(swa_vs),
            jnp.stack(mem_kp), jnp.stack(mem_vp))
```
